```python
import math
import jax, jax.numpy as jnp
from jax import lax
import numpy as np

D_MODEL = 1024
BATCH = 2
SEQ = 16384
DEPTH = 2

N_MIXERS = 2
N_GDN_LAYERS = (DEPTH + 1) // 2
N_SSD_LAYERS = DEPTH // 2
CONV_K = 4
EPS = 1e-6

GDN_H_QK = 8
GDN_H_V = 16
GDN_DK = 128
GDN_DV = 128
GDN_QK_W = GDN_H_QK * GDN_DK
GDN_V_W = GDN_H_V * GDN_DV
GDN_CONV_C = 2 * GDN_QK_W + GDN_V_W
GDN_IN_W = GDN_CONV_C + GDN_V_W + 2 * GDN_H_V
GDN_CHUNK = 64

SSD_D_INNER = 2 * D_MODEL
SSD_HEADDIM = 64
SSD_H = SSD_D_INNER // SSD_HEADDIM
SSD_N = 128
SSD_G = 8
SSD_R = SSD_H // SSD_G
SSD_CONV_C = SSD_D_INNER + 2 * SSD_G * SSD_N
SSD_IN_W = SSD_D_INNER + SSD_CONV_C + SSD_H
SSD_CHUNK = 128

kernel_name = "hybrid_gdn_mamba2_interleaved"


def _rmsnorm(x, w):
    xf = x.astype(jnp.float32)
    y = xf * lax.rsqrt(jnp.mean(xf * xf, axis=-1, keepdims=True) + EPS)
    return (y * w.astype(jnp.float32)).astype(x.dtype)


def _l2norm(x):
    xf = x.astype(jnp.float32)
    return xf * lax.rsqrt(jnp.sum(xf * xf, axis=-1, keepdims=True) + EPS)


def _causal_conv(x, w):
    k = w.shape[0]
    c = x.shape[-1]
    return lax.conv_general_dilated(
        x, w[:, None, :].astype(x.dtype), (1,), [(k - 1, 0)],
        dimension_numbers=("NWC", "WIO", "NWC"), feature_group_count=c)


def _chunk_gated_delta_rule(q, k, v, g, beta):
    b, t, h, dk = q.shape
    dv = v.shape[-1]
    c = GDN_CHUNK
    n = t // c
    f32 = jnp.float32
    def blk(a):
        return a.astype(f32).reshape(b, n, c, h, a.shape[-1]).transpose(0, 3, 1, 2, 4)
    q, k, v = blk(q), blk(k), blk(v)
    g = g.astype(f32).reshape(b, n, c, h).transpose(0, 3, 1, 2)
    beta = beta.astype(f32).reshape(b, n, c, h).transpose(0, 3, 1, 2)
    gc = jnp.cumsum(g, axis=-1)
    tril = jnp.tril(jnp.ones((c, c), dtype=bool))
    strict = jnp.tril(jnp.ones((c, c), dtype=bool), k=-1)
    L = jnp.exp(jnp.where(tril, gc[..., :, None] - gc[..., None, :], -jnp.inf))
    kb = k * beta[..., None]
    vb = v * beta[..., None]
    kkt = jnp.einsum("bhncd,bhnsd->bhncs", kb, k) * L
    a_mat = jnp.where(strict, kkt, 0.0) + jnp.eye(c, dtype=f32)
    rhs = jnp.concatenate([vb, kb * jnp.exp(gc)[..., None]], axis=-1)
    sol = lax.linalg.triangular_solve(a_mat, rhs, left_side=True, lower=True,
                                      unit_diagonal=True)
    u, w = sol[..., :dv], sol[..., dv:]
    attn = jnp.einsum("bhncd,bhnsd->bhncs", q, k) * L
    q_dec = q * jnp.exp(gc)[..., None]
    k_dec = k * jnp.exp(gc[..., -1:] - gc)[..., None]
    g_last = jnp.exp(gc[..., -1])

    def step(S, inp):
        qd, kd, u_c, w_c, a_c, gl = inp
        v_new = u_c - jnp.einsum("bhcd,bhde->bhce", w_c, S)
        o = jnp.einsum("bhcd,bhde->bhce", qd, S) + jnp.einsum("bhcs,bhse->bhce", a_c, v_new)
        S = S * gl[..., None, None] + jnp.einsum("bhcd,bhce->bhde", kd, v_new)
        return S, o

    mv = lambda a: jnp.moveaxis(a, 2, 0)
    S0 = jnp.zeros((b, h, dk, dv), f32)
    _, o = lax.scan(step, S0, (mv(q_dec), mv(k_dec), mv(u), mv(w), mv(attn), mv(g_last)))
    return o.transpose(1, 0, 3, 2, 4).reshape(b, t, h, dv)


def _gated_deltanet(hid, w_in, conv_w, a_log, dt_bias, norm_w, w_out):
    b, t, _ = hid.shape
    proj = hid @ w_in
    qkv, z, b_raw, a_raw = jnp.split(
        proj, [GDN_CONV_C, GDN_CONV_C + GDN_V_W, GDN_CONV_C + GDN_V_W + GDN_H_V], axis=-1)
    qkv = jax.nn.silu(_causal_conv(qkv, conv_w))
    q, k, v = jnp.split(qkv, [GDN_QK_W, 2 * GDN_QK_W], axis=-1)
    rep = GDN_H_V // GDN_H_QK
    q = jnp.repeat(_l2norm(q.reshape(b, t, GDN_H_QK, GDN_DK)) * (GDN_DK ** -0.5), rep, axis=2)
    k = jnp.repeat(_l2norm(k.reshape(b, t, GDN_H_QK, GDN_DK)), rep, axis=2)
    v = v.reshape(b, t, GDN_H_V, GDN_DV)
    beta = jax.nn.sigmoid(b_raw.astype(jnp.float32))
    g = -jnp.exp(a_log.astype(jnp.float32)) * jax.nn.softplus(
        a_raw.astype(jnp.float32) + dt_bias.astype(jnp.float32))
    o = _chunk_gated_delta_rule(q, k, v, g, beta).astype(hid.dtype)
    o = _rmsnorm(o, norm_w) * jax.nn.silu(z.reshape(b, t, GDN_H_V, GDN_DV))
    return o.reshape(b, t, GDN_V_W) @ w_out


def _ssd_chunked(X, adt, Bm, Cm):
    b, t, h, p = X.shape
    q = SSD_CHUNK
    n = t // q
    f32 = jnp.float32
    X = X.astype(f32).reshape(b, n, q, SSD_G, SSD_R, p)
    adt = adt.astype(f32).reshape(b, n, q, SSD_G, SSD_R).transpose(0, 3, 4, 1, 2)
    Bm = Bm.astype(f32).reshape(b, n, q, SSD_G, SSD_N)
    Cm = Cm.astype(f32).reshape(b, n, q, SSD_G, SSD_N)
    acs = jnp.cumsum(adt, axis=-1)
    tril = jnp.tril(jnp.ones((q, q), dtype=bool))
    Lm = jnp.exp(jnp.where(tril, acs[..., :, None] - acs[..., None, :], -jnp.inf))
    cb = jnp.einsum("bnlgd,bnsgd->bgnls", Cm, Bm)
    y_diag = jnp.einsum("bgrnls,bnsgrp->bnlgrp", cb[:, :, None] * Lm, X)
    decay_states = jnp.exp(acs[..., -1:] - acs)
    states = jnp.einsum("bnsgd,bgrns,bnsgrp->bngrpd", Bm, decay_states, X)
    chunk_decay = jnp.exp(acs[..., -1])

    def step(S, inp):
        st, dec = inp
        return S * dec[..., None, None] + st, S

    S0 = jnp.zeros((b, SSD_G, SSD_R, p, SSD_N), f32)
    _, prev = lax.scan(step, S0, (jnp.moveaxis(states, 1, 0), jnp.moveaxis(chunk_decay, 3, 0)))
    prev = jnp.moveaxis(prev, 0, 1)
    y_off = jnp.einsum("bnlgd,bngrpd,bgrnl->bnlgrp", Cm, prev, jnp.exp(acs))
    return (y_diag + y_off).reshape(b, t, h, p)


def _mamba2(hid, w_in, conv_w, conv_b, dt_bias, a_log, d_skip, norm_w, w_out):
    b, t, _ = hid.shape
    proj = hid @ w_in
    z, xbc, dt = jnp.split(proj, [SSD_D_INNER, SSD_D_INNER + SSD_CONV_C], axis=-1)
    xbc = jax.nn.silu(_causal_conv(xbc, conv_w) + conv_b)
    xs, Bm, Cm = jnp.split(xbc, [SSD_D_INNER, SSD_D_INNER + SSD_G * SSD_N], axis=-1)
    xs = xs.reshape(b, t, SSD_H, SSD_HEADDIM)
    Bm = Bm.reshape(b, t, SSD_G, SSD_N)
    Cm = Cm.reshape(b, t, SSD_G, SSD_N)
    dt = jax.nn.softplus(dt.astype(jnp.float32) + dt_bias.astype(jnp.float32))
    A = -jnp.exp(a_log.astype(jnp.float32))
    y = _ssd_chunked(xs.astype(jnp.float32) * dt[..., None], A * dt, Bm, Cm)
    y = (y + xs.astype(jnp.float32) * d_skip.astype(jnp.float32)[:, None]).astype(hid.dtype)
    y = y.reshape(b, t, SSD_D_INNER) * jax.nn.silu(z)
    gs = SSD_D_INNER // SSD_G
    y = _rmsnorm(y.reshape(b, t, SSD_G, gs), norm_w.reshape(SSD_G, gs)).reshape(b, t, SSD_D_INNER)
    return y @ w_out


def _dt_bias_init(key, n):
    dt = jnp.exp(jax.random.uniform(key, (n,), minval=math.log(1e-3), maxval=math.log(1e-1)))
    return dt + jnp.log(-jnp.expm1(-dt))


def setup_inputs(seed: int = 0) -> dict:
    key = jax.random.key(seed)
    ks = jax.random.split(key, 20)
    nA, nB = N_GDN_LAYERS, N_SSD_LAYERS
    nrm = jax.random.normal
    return {
        "x": nrm(ks[0], (BATCH, SEQ, D_MODEL), jnp.float32),
        "norm_w": 1.0 + 0.01 * nrm(ks[1], (DEPTH, D_MODEL), jnp.float32),
        "gdn_w_in": nrm(ks[2], (nA, D_MODEL, GDN_IN_W), jnp.float32) * D_MODEL ** -0.5,
        "gdn_conv_w": nrm(ks[3], (nA, CONV_K, GDN_CONV_C), jnp.float32) * 0.5,
        "gdn_a_log": jnp.log(jax.random.uniform(ks[4], (nA, GDN_H_V), minval=1.0, maxval=16.0)),
        "gdn_dt_bias": jnp.stack([_dt_bias_init(kk, GDN_H_V) for kk in jax.random.split(ks[5], nA)]),
        "gdn_norm_w": 1.0 + 0.01 * nrm(ks[6], (nA, GDN_DV), jnp.float32),
        "gdn_w_out": nrm(ks[7], (nA, GDN_V_W, D_MODEL), jnp.float32) * GDN_V_W ** -0.5,
        "ssd_w_in": nrm(ks[8], (nB, D_MODEL, SSD_IN_W), jnp.float32) * D_MODEL ** -0.5,
        "ssd_conv_w": nrm(ks[9], (nB, CONV_K, SSD_CONV_C), jnp.float32) * 0.5,
        "ssd_conv_b": 0.01 * nrm(ks[10], (nB, SSD_CONV_C), jnp.float32),
        "ssd_dt_bias": jnp.stack([_dt_bias_init(kk, SSD_H) for kk in jax.random.split(ks[11], nB)]),
        "ssd_a_log": jnp.log(jax.random.uniform(ks[12], (nB, SSD_H), minval=1.0, maxval=16.0)),
        "ssd_d": 1.0 + 0.01 * nrm(ks[13], (nB, SSD_H), jnp.float32),
        "ssd_norm_w": 1.0 + 0.01 * nrm(ks[14], (nB, SSD_D_INNER), jnp.float32),
        "ssd_w_out": nrm(ks[15], (nB, SSD_D_INNER, D_MODEL), jnp.float32) * SSD_D_INNER ** -0.5,
        "final_norm_w": 1.0 + 0.01 * nrm(ks[16], (D_MODEL,), jnp.float32),
    }


def reference(x, norm_w, gdn_w_in, gdn_conv_w, gdn_a_log, gdn_dt_bias, gdn_norm_w, gdn_w_out,
              ssd_w_in, ssd_conv_w, ssd_conv_b, ssd_dt_bias, ssd_a_log, ssd_d, ssd_norm_w,
              ssd_w_out, final_norm_w):
    for i in range(DEPTH):
        hid = _rmsnorm(x, norm_w[i])
        j = i // N_MIXERS
        if i % N_MIXERS == 0:
            x = x + _gated_deltanet(hid, gdn_w_in[j], gdn_conv_w[j], gdn_a_log[j],
                                    gdn_dt_bias[j], gdn_norm_w[j], gdn_w_out[j])
        else:
            x = x + _mamba2(hid, ssd_w_in[j], ssd_conv_w[j], ssd_conv_b[j], ssd_dt_bias[j],
                            ssd_a_log[j], ssd_d[j], ssd_norm_w[j], ssd_w_out[j])
    return _rmsnorm(x, final_norm_w)
```

```python
import functools

import jax
import jax.numpy as jnp
from jax import lax
from jax.experimental import pallas as pl
from jax.experimental.pallas import tpu as pltpu

F32 = jnp.float32
BF16 = jnp.bfloat16

EPS = 1e-6
CONV_K = 4
CONV_C = 4096
Z_W = 2048
SLAB_W = 128

GDN_H_QK = 8
GDN_H_V = 16
GDN_D = 128
GDN_CHUNK = 64

SSD_H = 32
SSD_P = 64
SSD_N = 128
SSD_G = 8
SSD_CHUNK = 128

HALO = 8
VMEM_LIMIT = 56 * 1024 * 1024

_NT = (((1,), (1,)), ((), ()))
_TN = (((0,), (0,)), ((), ()))


def _dot(a, b):
    return jnp.dot(a, b, preferred_element_type=F32)


def _sigmoid(x):
    return 1.0 / (1.0 + jnp.exp(-x))


def _silu(x):
    return x * _sigmoid(x)


def _softplus(x):
    return jnp.maximum(x, 0.0) + jnp.log1p(jnp.exp(-jnp.abs(x)))


def _rms_scale(x):
    return lax.rsqrt(jnp.mean(x * x, axis=-1, keepdims=True) + EPS)


def _chunk_cumsum(g, chunk):
    row = lax.broadcasted_iota(jnp.int32, g.shape, 0) & (chunk - 1)
    s = 1
    while s < chunk:
        g = g + jnp.where(row >= s, pltpu.roll(g, s, 0), 0.0)
        s *= 2
    return g


def _in_kernel(x_ref, nw_ref, wm_ref, ws_ref, cw_ref, cb_ref, p1_ref, p2_ref,
               conv_ref, z_ref, slab_ref, halo_ref, pbuf_ref, *, mode, tt, cw):
    @pl.when(pl.program_id(1) == 0)
    def _():
        halo_ref[...] = jnp.zeros_like(halo_ref)

    x = x_ref[...]
    h = (x * _rms_scale(x) * nw_ref[...]).astype(BF16)

    s = _dot(h, ws_ref[...])
    lane = lax.broadcasted_iota(jnp.int32, s.shape, 1)
    if mode == "gdn":
        n1, n2, chunk = GDN_H_V, 2 * GDN_H_V, GDN_CHUNK
        f1 = _sigmoid(s)
        f2 = -jnp.exp(p1_ref[...]) * _softplus(s + p2_ref[...])
    else:
        n1, n2, chunk = SSD_H, 2 * SSD_H, SSD_CHUNK
        f1 = _softplus(s + p2_ref[...])
        f2 = -jnp.exp(p1_ref[...]) * f1
    g = jnp.where((lane >= n1) & (lane < n2), f2, 0.0)
    slab_ref[...] = jnp.where(lane < n1, f1, _chunk_cumsum(g, chunk))

    for c in range(CONV_C // cw):
        cols = slice(c * cw, (c + 1) * cw)
        p = _dot(h, wm_ref[:, cols])
        pbuf_ref[0:HALO, :] = halo_ref[:, cols]
        pbuf_ref[HALO:HALO + tt, :] = p
        halo_ref[:, cols] = p[tt - HALO:tt, :]
        y = p * cw_ref[CONV_K - 1:CONV_K, cols]
        for j in range(CONV_K - 1):
            off = HALO - (CONV_K - 1) + j
            y = y + pbuf_ref[off:off + tt, :] * cw_ref[j:j + 1, cols]
        if mode == "ssd":
            y = y + cb_ref[:, cols]
        y = _silu(y)
        if mode == "gdn":
            for hh in range(cw // GDN_D):
                col0 = c * cw + hh * GDN_D
                yh = y[:, hh * GDN_D:(hh + 1) * GDN_D]
                if col0 < 2 * GDN_H_QK * GDN_D:
                    scale = lax.rsqrt(jnp.sum(yh * yh, axis=-1, keepdims=True) + EPS)
                    if col0 < GDN_H_QK * GDN_D:
                        scale = scale * (GDN_D ** -0.5)
                    yh = yh * scale
                conv_ref[:, col0:col0 + GDN_D] = yh.astype(BF16)
        else:
            conv_ref[:, cols] = y.astype(BF16)

    for c in range(Z_W // cw):
        p = _dot(h, wm_ref[:, CONV_C + c * cw:CONV_C + (c + 1) * cw])
        z_ref[:, c * cw:(c + 1) * cw] = _silu(p).astype(BF16)


def _in_proj(x, nw, w_main, w_small, conv_w, conv_b, p1, p2, *, mode, tt=512, cw=1024):
    b, t, d = x.shape
    const = lambda i, j: (0, 0)
    tile = lambda i, j: (i, j, 0)
    return pl.pallas_call(
        functools.partial(_in_kernel, mode=mode, tt=tt, cw=cw),
        grid=(b, t // tt),
        in_specs=[
            pl.BlockSpec((None, tt, d), tile),
            pl.BlockSpec((1, d), const),
            pl.BlockSpec((d, CONV_C + Z_W), const, pipeline_mode=pl.Buffered(1)),
            pl.BlockSpec((d, SLAB_W), const),
            pl.BlockSpec((CONV_K, CONV_C), const),
            pl.BlockSpec((1, CONV_C), const),
            pl.BlockSpec((1, SLAB_W), const),
            pl.BlockSpec((1, SLAB_W), const),
        ],
        out_specs=[
            pl.BlockSpec((None, tt, CONV_C), tile),
            pl.BlockSpec((None, tt, Z_W), tile),
            pl.BlockSpec((None, tt, SLAB_W), tile),
        ],
        out_shape=[
            jax.ShapeDtypeStruct((b, t, CONV_C), BF16),
            jax.ShapeDtypeStruct((b, t, Z_W), BF16),
            jax.ShapeDtypeStruct((b, t, SLAB_W), F32),
        ],
        scratch_shapes=[
            pltpu.VMEM((HALO, CONV_C), F32),
            pltpu.VMEM((HALO + tt, cw), F32),
        ],
        compiler_params=pltpu.CompilerParams(
            dimension_semantics=("arbitrary", "arbitrary"), vmem_limit_bytes=VMEM_LIMIT),
        name=f"in_proj_{mode}",
    )(x, nw, w_main, w_small, conv_w, conv_b, p1, p2)


def _out_kernel(y_ref, w_ref, x_ref, fw_ref, o_ref, *, final):
    acc = x_ref[...] + _dot(y_ref[...], w_ref[...])
    if final:
        acc = acc * _rms_scale(acc) * fw_ref[...]
    o_ref[...] = acc


def _out_proj(y, w, x, fw, *, final, tt=512):
    b, t, d = x.shape
    k = y.shape[-1]
    const = lambda i, j: (0, 0)
    tile = lambda i, j: (i, j, 0)
    return pl.pallas_call(
        functools.partial(_out_kernel, final=final),
        grid=(b, t // tt),
        in_specs=[
            pl.BlockSpec((None, tt, k), tile),
            pl.BlockSpec((k, d), const),
            pl.BlockSpec((None, tt, d), tile),
            pl.BlockSpec((1, d), const),
        ],
        out_specs=pl.BlockSpec((None, tt, d), tile),
        out_shape=jax.ShapeDtypeStruct((b, t, d), F32),
        compiler_params=pltpu.CompilerParams(
            dimension_semantics=("arbitrary", "arbitrary"), vmem_limit_bytes=VMEM_LIMIT),
        name="out_proj_final" if final else "out_proj",
    )(y, w, x, fw)


def _gdn_kernel(q_ref, k_ref, v_ref, z_ref, gb_ref, nw_ref, y_ref, s_ref, *, tc):
    C, D = GDN_CHUNK, GDN_D

    @pl.when(pl.program_id(2) == 0)
    def _():
        s_ref[...] = jnp.zeros_like(s_ref)

    row = lax.broadcasted_iota(jnp.int32, (C, C), 0)
    col = lax.broadcasted_iota(jnp.int32, (C, C), 1)
    tril, strict, eye = row >= col, row > col, row == col
    nw = nw_ref[...]

    def chunk(c, carry):
        rows = pl.ds(pl.multiple_of(c * C, C), C)
        kb = k_ref[rows, :]
        qf = q_ref[rows, :].astype(F32)
        kf = kb.astype(F32)
        qk_kk = lax.dot_general(jnp.concatenate([qf, kf], axis=0).astype(BF16), kb, _NT,
                                preferred_element_type=F32)
        qk, kk = qk_kk[:C], qk_kk[C:]
        g4 = gb_ref[rows, :]
        for e in range(2):
            lanes = slice(e * D, (e + 1) * D)
            beta, gc = g4[:, e:e + 1], g4[:, 2 + e:3 + e]
            gc_row = jnp.sum(jnp.where(eye, gc, 0.0), axis=0, keepdims=True)
            decay = jnp.exp(jnp.where(tril, gc - gc_row, -jnp.inf))
            nmat = jnp.where(strict, kk * decay, 0.0) * beta
            attn = qk * decay
            eg = jnp.exp(gc)
            vf = v_ref[rows, lanes].astype(F32)
            rhs = jnp.concatenate([vf * beta, kf * (beta * eg)], axis=1)
            pm = nmat.astype(BF16)
            sol = rhs - _dot(pm, rhs.astype(BF16))
            for _ in range(5):
                pm = _dot(pm, pm).astype(BF16)
                sol = sol + _dot(pm, sol.astype(BF16))
            u, w = sol[:, :D], sol[:, D:]
            state = s_ref[e]
            ws = _dot(jnp.concatenate([w, qf * eg], axis=0).astype(BF16), state.astype(BF16))
            v_new = (u - ws[:C]).astype(BF16)
            o = ws[C:] + _dot(attn.astype(BF16), v_new)
            g_last = gc[C - 1:C, :]
            kd = (kf * jnp.exp(g_last - gc)).astype(BF16)
            s_ref[e] = state * jnp.exp(g_last) + lax.dot_general(
                kd, v_new, _TN, preferred_element_type=F32)
            gate = z_ref[rows, lanes].astype(F32)
            y_ref[rows, lanes] = (o * _rms_scale(o) * nw * gate).astype(BF16)
        return carry

    lax.fori_loop(0, tc // C, chunk, 0)


def _gdn_scan(qkv, zs, gb, nw, *, tc=512):
    b, t, _ = qkv.shape
    hq, d = GDN_H_QK, GDN_D
    return pl.pallas_call(
        functools.partial(_gdn_kernel, tc=tc),
        grid=(b, hq, t // tc),
        in_specs=[
            pl.BlockSpec((None, tc, d), lambda i, h, j: (i, j, h)),
            pl.BlockSpec((None, tc, d), lambda i, h, j: (i, j, hq + h)),
            pl.BlockSpec((None, tc, 2 * d), lambda i, h, j: (i, j, hq + h)),
            pl.BlockSpec((None, tc, 2 * d), lambda i, h, j: (i, j, h)),
            pl.BlockSpec((None, None, tc, 4), lambda i, h, j: (i, h, j, 0)),
            pl.BlockSpec((1, d), lambda i, h, j: (0, 0)),
        ],
        out_specs=pl.BlockSpec((None, tc, 2 * d), lambda i, h, j: (i, j, h)),
        out_shape=jax.ShapeDtypeStruct((b, t, GDN_H_V * d), BF16),
        scratch_shapes=[pltpu.VMEM((2, d, d), F32)],
        compiler_params=pltpu.CompilerParams(
            dimension_semantics=("arbitrary", "arbitrary", "arbitrary"),
            vmem_limit_bytes=VMEM_LIMIT),
        name="gdn_scan",
    )(qkv, qkv, qkv, zs, gb, nw)


def _ssd_kernel(xs_ref, b_ref, c_ref, z_ref, sl_ref, d_ref, nw_ref, y_ref, st_ref, *, tc):
    Q, W = SSD_CHUNK, 2 * SSD_P

    @pl.when(pl.program_id(2) == 0)
    def _():
        st_ref[...] = jnp.zeros_like(st_ref)

    row = lax.broadcasted_iota(jnp.int32, (Q, Q), 0)
    col = lax.broadcasted_iota(jnp.int32, (Q, Q), 1)
    tril, eye = row >= col, row == col
    lo = lax.broadcasted_iota(jnp.int32, (Q, W), 1) < SSD_P
    nw = nw_ref[...]
    dskip = d_ref[...]

    def chunk(c, carry):
        rows = pl.ds(pl.multiple_of(c * Q, Q), Q)
        bm = b_ref[rows, :]
        cm = c_ref[rows, :]
        xs = xs_ref[rows, :].astype(F32)
        s8 = sl_ref[rows, :]
        cb = lax.dot_general(cm, bm, _NT, preferred_element_type=F32)
        state = st_ref[...]
        y_off = _dot(cm, state.astype(BF16))
        ys, xds, cdecs = [], [], []
        for p in range(2):
            lanes = slice(p * W, (p + 1) * W)
            xp = xs[:, lanes]
            dts = [s8[:, 2 * p + i:2 * p + i + 1] for i in range(2)]
            acs = [s8[:, 4 + 2 * p + i:4 + 2 * p + i + 1] for i in range(2)]
            acl = jnp.where(lo, acs[0], acs[1])
            xdt = xp * jnp.where(lo, dts[0], dts[1])
            xdt_b = xdt.astype(BF16)
            yd = []
            for ac in acs:
                ac_row = jnp.sum(jnp.where(eye, ac, 0.0), axis=0, keepdims=True)
                decay = jnp.exp(jnp.where(tril, ac - ac_row, -jnp.inf))
                yd.append(_dot((cb * decay).astype(BF16), xdt_b))
            last = jnp.where(lo[0:1], acs[0][Q - 1:Q, :], acs[1][Q - 1:Q, :])
            ys.append(jnp.where(lo, yd[0], yd[1]) + y_off[:, lanes] * jnp.exp(acl)
                      + xp * dskip[:, lanes])
            xds.append((xdt * jnp.exp(last - acl)).astype(BF16))
            cdecs.append(jnp.exp(last))
        st_ref[...] = state * jnp.concatenate(cdecs, axis=1) + lax.dot_general(
            bm, jnp.concatenate(xds, axis=1), _TN, preferred_element_type=F32)
        y = jnp.concatenate(ys, axis=1) * z_ref[rows, :].astype(F32)
        y_ref[rows, :] = (y * _rms_scale(y) * nw).astype(BF16)
        return carry

    lax.fori_loop(0, tc // Q, chunk, 0)


def _ssd_scan(xbc, zs, sl, dskip, nw, *, tc=512):
    b, t, _ = xbc.shape
    g, n = SSD_G, SSD_N
    gw = SSD_H // SSD_G * SSD_P
    xw = SSD_H * SSD_P // n
    return pl.pallas_call(
        functools.partial(_ssd_kernel, tc=tc),
        grid=(b, g, t // tc),
        in_specs=[
            pl.BlockSpec((None, tc, gw), lambda i, h, j: (i, j, h)),
            pl.BlockSpec((None, tc, n), lambda i, h, j: (i, j, xw + h)),
            pl.BlockSpec((None, tc, n), lambda i, h, j: (i, j, xw + g + h)),
            pl.BlockSpec((None, tc, gw), lambda i, h, j: (i, j, h)),
            pl.BlockSpec((None, None, tc, 8), lambda i, h, j: (i, h, j, 0)),
            pl.BlockSpec((1, gw), lambda i, h, j: (0, h)),
            pl.BlockSpec((1, gw), lambda i, h, j: (0, h)),
        ],
        out_specs=pl.BlockSpec((None, tc, gw), lambda i, h, j: (i, j, h)),
        out_shape=jax.ShapeDtypeStruct((b, t, SSD_H * SSD_P), BF16),
        scratch_shapes=[pltpu.VMEM((n, gw), F32)],
        compiler_params=pltpu.CompilerParams(
            dimension_semantics=("arbitrary", "arbitrary", "arbitrary"),
            vmem_limit_bytes=VMEM_LIMIT),
        name="ssd_scan",
    )(xbc, xbc, xbc, zs, sl, dskip, nw)


def _lane_row(pieces):
    row = jnp.zeros((SLAB_W,), F32)
    for off, v in pieces:
        row = row.at[off:off + v.shape[0]].set(v.astype(F32))
    return row[None, :]


def _per_head(slab, kinds, groups, per):
    b, t, _ = slab.shape
    a = slab[..., :kinds * groups * per].reshape(b, t, kinds, groups, per)
    return a.transpose(0, 3, 1, 2, 4).reshape(b, groups, t, kinds * per)


def kernel(x, norm_w, gdn_w_in, gdn_conv_w, gdn_a_log, gdn_dt_bias, gdn_norm_w, gdn_w_out,
           ssd_w_in, ssd_conv_w, ssd_conv_b, ssd_dt_bias, ssd_a_log, ssd_d, ssd_norm_w,
           ssd_w_out, final_norm_w):
    d = x.shape[-1]
    hv, h = GDN_H_V, SSD_H

    w = gdn_w_in[0]
    w_small = jnp.pad(w[:, CONV_C + Z_W:], ((0, 0), (0, SLAB_W - 2 * hv))).astype(BF16)
    qkv, zs, slab = _in_proj(
        x, norm_w[0:1], w[:, :CONV_C + Z_W].astype(BF16), w_small, gdn_conv_w[0],
        jnp.zeros((1, CONV_C), F32), _lane_row([(hv, gdn_a_log[0])]),
        _lane_row([(hv, gdn_dt_bias[0])]), mode="gdn")
    y = _gdn_scan(qkv, zs, _per_head(slab, 2, GDN_H_QK, 2), gdn_norm_w[0][None, :])
    x = _out_proj(y, gdn_w_out[0].astype(BF16), x, final_norm_w[None, :], final=False)

    w = ssd_w_in[0]
    w_dt = w[:, Z_W + CONV_C:]
    w_main = jnp.concatenate([w[:, Z_W:Z_W + CONV_C], w[:, :Z_W]], axis=1).astype(BF16)
    w_small = jnp.pad(jnp.concatenate([w_dt, w_dt], axis=1),
                      ((0, 0), (0, SLAB_W - 2 * h))).astype(BF16)
    xbc, zs, slab = _in_proj(
        x, norm_w[1:2], w_main, w_small, ssd_conv_w[0], ssd_conv_b[0][None, :],
        _lane_row([(h, ssd_a_log[0])]),
        _lane_row([(0, ssd_dt_bias[0]), (h, ssd_dt_bias[0])]), mode="ssd")
    dskip = jnp.repeat(ssd_d[0].astype(F32), SSD_P)[None, :]
    y = _ssd_scan(xbc, zs, _per_head(slab, 2, SSD_G, SSD_H // SSD_G), dskip,
                  ssd_norm_w[0][None, :])
    return _out_proj(y, ssd_w_out[0].astype(BF16), x, final_norm_w[None, :], final=True)
```

```python
import functools

import jax
import jax.numpy as jnp
from jax import lax
from jax.experimental import pallas as pl
from jax.experimental.pallas import tpu as pltpu

F32 = jnp.float32
BF16 = jnp.bfloat16

EPS = 1e-6
CONV_K = 4
CONV_C = 4096
Z_W = 2048
SLAB_W = 128

GDN_H_QK = 8
GDN_H_V = 16
GDN_D = 128
GDN_CHUNK = 64

SSD_H = 32
SSD_P = 64
SSD_N = 128
SSD_G = 8
SSD_CHUNK = 128

HALO = 8
VMEM_LIMIT = 56 * 1024 * 1024

_NT = (((1,), (1,)), ((), ()))
_TN = (((0,), (0,)), ((), ()))


def _dot(a, b):
    return jnp.dot(a, b, preferred_element_type=F32)


def _sigmoid(x):
    return 1.0 / (1.0 + jnp.exp(-x))


def _silu(x):
    return x * _sigmoid(x)


def _softplus(x):
    return jnp.maximum(x, 0.0) + jnp.log1p(jnp.exp(-jnp.abs(x)))


def _rms_scale(x):
    return lax.rsqrt(jnp.mean(x * x, axis=-1, keepdims=True) + EPS)


def _chunk_cumsum(g, chunk):
    row = lax.broadcasted_iota(jnp.int32, g.shape, 0) & (chunk - 1)
    s = 1
    while s < chunk:
        g = g + jnp.where(row >= s, pltpu.roll(g, s, 0), 0.0)
        s *= 2
    return g


def _in_kernel(x_ref, nw_ref, wm_ref, ws_ref, cw_ref, cb_ref, p1_ref, p2_ref,
               conv_ref, z_ref, slab_ref, halo_ref, pbuf_ref, *, mode, tt, cw):
    @pl.when(pl.program_id(1) == 0)
    def _():
        halo_ref[...] = jnp.zeros_like(halo_ref)

    x = x_ref[...]
    h = (x * _rms_scale(x) * nw_ref[...]).astype(BF16)

    s = _dot(h, ws_ref[...])
    lane = lax.broadcasted_iota(jnp.int32, s.shape, 1)
    if mode == "gdn":
        n1, n2, chunk = GDN_H_V, 2 * GDN_H_V, GDN_CHUNK
        f1 = _sigmoid(s)
        f2 = -jnp.exp(p1_ref[...]) * _softplus(s + p2_ref[...])
    else:
        n1, n2, chunk = SSD_H, 2 * SSD_H, SSD_CHUNK
        f1 = _softplus(s + p2_ref[...])
        f2 = -jnp.exp(p1_ref[...]) * f1
    g = jnp.where((lane >= n1) & (lane < n2), f2, 0.0)
    slab_ref[...] = jnp.where(lane < n1, f1, _chunk_cumsum(g, chunk))

    for c in range(CONV_C // cw):
        cols = slice(c * cw, (c + 1) * cw)
        p = _dot(h, wm_ref[:, cols])
        pbuf_ref[0:HALO, :] = halo_ref[:, cols]
        pbuf_ref[HALO:HALO + tt, :] = p
        halo_ref[:, cols] = p[tt - HALO:tt, :]
        y = p * cw_ref[CONV_K - 1:CONV_K, cols]
        for j in range(CONV_K - 1):
            off = HALO - (CONV_K - 1) + j
            y = y + pbuf_ref[off:off + tt, :] * cw_ref[j:j + 1, cols]
        if mode == "ssd":
            y = y + cb_ref[:, cols]
        y = _silu(y)
        if mode == "gdn":
            for hh in range(cw // GDN_D):
                col0 = c * cw + hh * GDN_D
                yh = y[:, hh * GDN_D:(hh + 1) * GDN_D]
                if col0 < 2 * GDN_H_QK * GDN_D:
                    scale = lax.rsqrt(jnp.sum(yh * yh, axis=-1, keepdims=True) + EPS)
                    if col0 < GDN_H_QK * GDN_D:
                        scale = scale * (GDN_D ** -0.5)
                    yh = yh * scale
                conv_ref[:, col0:col0 + GDN_D] = yh.astype(BF16)
        else:
            conv_ref[:, cols] = y.astype(BF16)

    for c in range(Z_W // cw):
        p = _dot(h, wm_ref[:, CONV_C + c * cw:CONV_C + (c + 1) * cw])
        z_ref[:, c * cw:(c + 1) * cw] = _silu(p).astype(BF16)


def _in_proj(x, nw, w_main, w_small, conv_w, conv_b, p1, p2, *, mode, tt=512, cw=1024):
    b, t, d = x.shape
    const = lambda i, j: (0, 0)
    tile = lambda i, j: (i, j, 0)
    return pl.pallas_call(
        functools.partial(_in_kernel, mode=mode, tt=tt, cw=cw),
        grid=(b, t // tt),
        in_specs=[
            pl.BlockSpec((None, tt, d), tile),
            pl.BlockSpec((1, d), const),
            pl.BlockSpec((d, CONV_C + Z_W), const, pipeline_mode=pl.Buffered(1)),
            pl.BlockSpec((d, SLAB_W), const),
            pl.BlockSpec((CONV_K, CONV_C), const),
            pl.BlockSpec((1, CONV_C), const),
            pl.BlockSpec((1, SLAB_W), const),
            pl.BlockSpec((1, SLAB_W), const),
        ],
        out_specs=[
            pl.BlockSpec((None, tt, CONV_C), tile),
            pl.BlockSpec((None, tt, Z_W), tile),
            pl.BlockSpec((None, tt, SLAB_W), tile),
        ],
        out_shape=[
            jax.ShapeDtypeStruct((b, t, CONV_C), BF16),
            jax.ShapeDtypeStruct((b, t, Z_W), BF16),
            jax.ShapeDtypeStruct((b, t, SLAB_W), F32),
        ],
        scratch_shapes=[
            pltpu.VMEM((HALO, CONV_C), F32),
            pltpu.VMEM((HALO + tt, cw), F32),
        ],
        compiler_params=pltpu.CompilerParams(
            dimension_semantics=("arbitrary", "arbitrary"), vmem_limit_bytes=VMEM_LIMIT),
        name=f"in_proj_{mode}",
    )(x, nw, w_main, w_small, conv_w, conv_b, p1, p2)


def _out_kernel(y_ref, w_ref, x_ref, fw_ref, o_ref, *, final):
    acc = x_ref[...] + _dot(y_ref[...], w_ref[...])
    if final:
        acc = acc * _rms_scale(acc) * fw_ref[...]
    o_ref[...] = acc


def _out_proj(y, w, x, fw, *, final, tt=512):
    b, t, d = x.shape
    k = y.shape[-1]
    const = lambda i, j: (0, 0)
    tile = lambda i, j: (i, j, 0)
    return pl.pallas_call(
        functools.partial(_out_kernel, final=final),
        grid=(b, t // tt),
        in_specs=[
            pl.BlockSpec((None, tt, k), tile),
            pl.BlockSpec((k, d), const),
            pl.BlockSpec((None, tt, d), tile),
            pl.BlockSpec((1, d), const),
        ],
        out_specs=pl.BlockSpec((None, tt, d), tile),
        out_shape=jax.ShapeDtypeStruct((b, t, d), F32),
        compiler_params=pltpu.CompilerParams(
            dimension_semantics=("arbitrary", "arbitrary"), vmem_limit_bytes=VMEM_LIMIT),
        name="out_proj_final" if final else "out_proj",
    )(y, w, x, fw)


def _gdn_kernel(q_ref, k_ref, v_ref, z_ref, gb_ref, nw_ref, y_ref, s_ref, *, tc, nb):
    C, D = GDN_CHUNK, GDN_D

    @pl.when(pl.program_id(1) == 0)
    def _():
        s_ref[...] = jnp.zeros_like(s_ref)

    row = lax.broadcasted_iota(jnp.int32, (C, 2 * C), 0)
    lane = lax.broadcasted_iota(jnp.int32, (C, 2 * C), 1)
    col = lane & (C - 1)
    tril, strict, eye = row >= col, row > col, row == col
    hi = lane >= C
    ident_hi = jnp.where(hi & eye, 1.0, 0.0)
    zero_rows = jnp.zeros((C, 2 * D), BF16)
    nw = nw_ref[...]
    nc = tc // C
    pairs = [(c, b) for c in range(nc) for b in range(nb)]
    chains = [(c, b, e) for c in range(nc) for b in range(nb) for e in range(2)]
    rows = lambda c: slice(c * C, (c + 1) * C)
    lanes = lambda e: slice(e * D, (e + 1) * D)

    kf, qk, kk = {}, {}, {}
    for c, b in pairs:
        kb = k_ref[b, rows(c), :]
        qb = q_ref[b, rows(c), :]
        kf[c, b] = kb.astype(F32)
        qk_kk = lax.dot_general(jnp.concatenate([qb, kb], axis=0),
                                jnp.concatenate([kb, kb], axis=0), _NT,
                                preferred_element_type=F32)
        qk[c, b], kk[c, b] = qk_kk[:C], qk_kk[C:]

    beta, gc, attn, zk = {}, {}, {}, {}
    for c, b, e in chains:
        g4 = gb_ref[b, rows(c), :]
        beta[c, b, e], gc[c, b, e] = g4[:, e:e + 1], g4[:, 2 + e:3 + e]
        gcv = gc[c, b, e]
        gc_row = jnp.sum(jnp.where(eye, gcv, 0.0), axis=0, keepdims=True)
        decay = jnp.exp(jnp.where(tril, gcv - gc_row, -jnp.inf))
        nmat = jnp.where(strict, kk[c, b] * decay, 0.0) * beta[c, b, e]
        attn[c, b, e] = (qk[c, b] * decay)[:, :C].astype(BF16)
        zk[c, b, e] = jnp.where(hi, ident_hi, -nmat)

    for _ in range(6):
        for ch in chains:
            zb = zk[ch].astype(BF16)
            zk[ch] = _dot(zb[:, :C], zb) + jnp.where(hi, zk[ch], 0.0)

    eg, u, wq = {}, {}, {}
    for c, b, e in chains:
        bt = beta[c, b, e]
        eg[c, b, e] = jnp.exp(gc[c, b, e])
        vf = v_ref[b, rows(c), lanes(e)].astype(F32)
        rhs = jnp.concatenate([vf * bt, kf[c, b] * (bt * eg[c, b, e])], axis=1).astype(BF16)
        sol = _dot(zk[c, b, e].astype(BF16), jnp.concatenate([zero_rows, rhs], axis=0))
        u[c, b, e] = sol[:, :D]
        qd = q_ref[b, rows(c), :].astype(F32) * eg[c, b, e]
        wq[c, b, e] = jnp.concatenate([sol[:, D:], qd], axis=0).astype(BF16)

    states = {(b, e): s_ref[b, e] for b in range(nb) for e in range(2)}
    for c in range(nc):
        be = [(b, e) for b in range(nb) for e in range(2)]
        ws = {k: _dot(wq[(c,) + k], states[k].astype(BF16)) for k in be}
        v_new = {k: (u[(c,) + k] - ws[k][:C]).astype(BF16) for k in be}
        for b, e in be:
            gcv = gc[c, b, e]
            g_last = gcv[C - 1:C, :]
            kd = (kf[c, b] * jnp.exp(g_last - gcv)).astype(BF16)
            states[b, e] = states[b, e] * jnp.exp(g_last) + lax.dot_general(
                kd, v_new[b, e], _TN, preferred_element_type=F32)
        for b, e in be:
            o = ws[b, e][C:] + _dot(attn[c, b, e], v_new[b, e])
            gate = z_ref[b, rows(c), lanes(e)].astype(F32)
            y_ref[b, rows(c), lanes(e)] = (o * _rms_scale(o) * nw * gate).astype(BF16)

    for k, s in states.items():
        s_ref[k[0], k[1]] = s


def _gdn_scan(qkv, zs, gb, nw, *, tc=256):
    b, t, _ = qkv.shape
    hq, d = GDN_H_QK, GDN_D
    return pl.pallas_call(
        functools.partial(_gdn_kernel, tc=tc, nb=b),
        grid=(hq, t // tc),
        in_specs=[
            pl.BlockSpec((b, tc, d), lambda h, j: (0, j, h)),
            pl.BlockSpec((b, tc, d), lambda h, j: (0, j, hq + h)),
            pl.BlockSpec((b, tc, 2 * d), lambda h, j: (0, j, hq + h)),
            pl.BlockSpec((b, tc, 2 * d), lambda h, j: (0, j, h)),
            pl.BlockSpec((b, None, tc, 4), lambda h, j: (0, h, j, 0)),
            pl.BlockSpec((1, d), lambda h, j: (0, 0)),
        ],
        out_specs=pl.BlockSpec((b, tc, 2 * d), lambda h, j: (0, j, h)),
        out_shape=jax.ShapeDtypeStruct((b, t, GDN_H_V * d), BF16),
        scratch_shapes=[pltpu.VMEM((b, 2, d, d), F32)],
        compiler_params=pltpu.CompilerParams(
            dimension_semantics=("arbitrary", "arbitrary"), vmem_limit_bytes=VMEM_LIMIT),
        name="gdn_scan",
    )(qkv, qkv, qkv, zs, gb, nw)


def _ssd_kernel(xs_ref, b_ref, c_ref, z_ref, sl_ref, d_ref, nw_ref, y_ref, st_ref, *, tc, nb):
    Q, W = SSD_CHUNK, 2 * SSD_P

    @pl.when(pl.program_id(1) == 0)
    def _():
        st_ref[...] = jnp.zeros_like(st_ref)

    row = lax.broadcasted_iota(jnp.int32, (Q, Q), 0)
    col = lax.broadcasted_iota(jnp.int32, (Q, Q), 1)
    tril, eye = row >= col, row == col
    lo = lax.broadcasted_iota(jnp.int32, (Q, W), 1) < SSD_P
    nw = nw_ref[...]
    dskip = d_ref[...]
    nc = tc // Q
    pairs = [(c, b) for c in range(nc) for b in range(nb)]
    rows = lambda c: slice(c * Q, (c + 1) * Q)
    lanes = lambda p: slice(p * W, (p + 1) * W)

    cb = {}
    for c, b in pairs:
        cb[c, b] = lax.dot_general(c_ref[b, rows(c), :], b_ref[b, rows(c), :], _NT,
                                   preferred_element_type=F32)

    acl, xdt_b, mats, cdec, xd = {}, {}, {}, {}, {}
    for c, b in pairs:
        s8 = sl_ref[b, rows(c), :]
        xds, cds = [], []
        for p in range(2):
            dts = [s8[:, 2 * p + i:2 * p + i + 1] for i in range(2)]
            acs = [s8[:, 4 + 2 * p + i:4 + 2 * p + i + 1] for i in range(2)]
            acl[c, b, p] = jnp.where(lo, acs[0], acs[1])
            xdt = xs_ref[b, rows(c), lanes(p)].astype(F32) * jnp.where(lo, dts[0], dts[1])
            xdt_b[c, b, p] = xdt.astype(BF16)
            for i, ac in enumerate(acs):
                ac_row = jnp.sum(jnp.where(eye, ac, 0.0), axis=0, keepdims=True)
                decay = jnp.exp(jnp.where(tril, ac - ac_row, -jnp.inf))
                mats[c, b, p, i] = (cb[c, b] * decay).astype(BF16)
            last = jnp.where(lo[0:1], acs[0][Q - 1:Q, :], acs[1][Q - 1:Q, :])
            xds.append((xdt * jnp.exp(last - acl[c, b, p])).astype(BF16))
            cds.append(jnp.exp(last))
        xd[c, b] = jnp.concatenate(xds, axis=1)
        cdec[c, b] = jnp.concatenate(cds, axis=1)

    yd = {k: _dot(m, xdt_b[k[:3]]) for k, m in mats.items()}
    upd = {k: lax.dot_general(b_ref[k[1], rows(k[0]), :], xd[k], _TN, preferred_element_type=F32)
           for k in pairs}

    prev = {}
    for b in range(nb):
        state = st_ref[b]
        for c in range(nc):
            prev[c, b] = state.astype(BF16)
            state = state * cdec[c, b] + upd[c, b]
        st_ref[b] = state
    y_off = {k: _dot(c_ref[k[1], rows(k[0]), :], prev[k]) for k in pairs}

    for c, b in pairs:
        ys = []
        for p in range(2):
            xp = xs_ref[b, rows(c), lanes(p)].astype(F32)
            ys.append(jnp.where(lo, yd[c, b, p, 0], yd[c, b, p, 1])
                      + y_off[c, b][:, lanes(p)] * jnp.exp(acl[c, b, p]) + xp * dskip[:, lanes(p)])
        y = jnp.concatenate(ys, axis=1) * z_ref[b, rows(c), :].astype(F32)
        y_ref[b, rows(c), :] = (y * _rms_scale(y) * nw).astype(BF16)


def _ssd_scan(xbc, zs, sl, dskip, nw, *, tc=256):
    b, t, _ = xbc.shape
    g, n = SSD_G, SSD_N
    gw = SSD_H // SSD_G * SSD_P
    xw = SSD_H * SSD_P // n
    return pl.pallas_call(
        functools.partial(_ssd_kernel, tc=tc, nb=b),
        grid=(g, t // tc),
        in_specs=[
            pl.BlockSpec((b, tc, gw), lambda h, j: (0, j, h)),
            pl.BlockSpec((b, tc, n), lambda h, j: (0, j, xw + h)),
            pl.BlockSpec((b, tc, n), lambda h, j: (0, j, xw + g + h)),
            pl.BlockSpec((b, tc, gw), lambda h, j: (0, j, h)),
            pl.BlockSpec((b, None, tc, 8), lambda h, j: (0, h, j, 0)),
            pl.BlockSpec((1, gw), lambda h, j: (0, h)),
            pl.BlockSpec((1, gw), lambda h, j: (0, h)),
        ],
        out_specs=pl.BlockSpec((b, tc, gw), lambda h, j: (0, j, h)),
        out_shape=jax.ShapeDtypeStruct((b, t, SSD_H * SSD_P), BF16),
        scratch_shapes=[pltpu.VMEM((b, n, gw), F32)],
        compiler_params=pltpu.CompilerParams(
            dimension_semantics=("arbitrary", "arbitrary"), vmem_limit_bytes=VMEM_LIMIT),
        name="ssd_scan",
    )(xbc, xbc, xbc, zs, sl, dskip, nw)


def _lane_row(pieces):
    row = jnp.zeros((SLAB_W,), F32)
    for off, v in pieces:
        row = row.at[off:off + v.shape[0]].set(v.astype(F32))
    return row[None, :]


def _per_head(slab, kinds, groups, per):
    b, t, _ = slab.shape
    a = slab[..., :kinds * groups * per].reshape(b, t, kinds, groups, per)
    return a.transpose(0, 3, 1, 2, 4).reshape(b, groups, t, kinds * per)


def kernel(x, norm_w, gdn_w_in, gdn_conv_w, gdn_a_log, gdn_dt_bias, gdn_norm_w, gdn_w_out,
           ssd_w_in, ssd_conv_w, ssd_conv_b, ssd_dt_bias, ssd_a_log, ssd_d, ssd_norm_w,
           ssd_w_out, final_norm_w):
    d = x.shape[-1]
    hv, h = GDN_H_V, SSD_H

    w = gdn_w_in[0]
    w_small = jnp.pad(w[:, CONV_C + Z_W:], ((0, 0), (0, SLAB_W - 2 * hv))).astype(BF16)
    qkv, zs, slab = _in_proj(
        x, norm_w[0:1], w[:, :CONV_C + Z_W].astype(BF16), w_small, gdn_conv_w[0],
        jnp.zeros((1, CONV_C), F32), _lane_row([(hv, gdn_a_log[0])]),
        _lane_row([(hv, gdn_dt_bias[0])]), mode="gdn")
    y = _gdn_scan(qkv, zs, _per_head(slab, 2, GDN_H_QK, 2), gdn_norm_w[0][None, :])
    x = _out_proj(y, gdn_w_out[0].astype(BF16), x, final_norm_w[None, :], final=False)

    w = ssd_w_in[0]
    w_dt = w[:, Z_W + CONV_C:]
    w_main = jnp.concatenate([w[:, Z_W:Z_W + CONV_C], w[:, :Z_W]], axis=1).astype(BF16)
    w_small = jnp.pad(jnp.concatenate([w_dt, w_dt], axis=1),
                      ((0, 0), (0, SLAB_W - 2 * h))).astype(BF16)
    xbc, zs, slab = _in_proj(
        x, norm_w[1:2], w_main, w_small, ssd_conv_w[0], ssd_conv_b[0][None, :],
        _lane_row([(h, ssd_a_log[0])]),
        _lane_row([(0, ssd_dt_bias[0]), (h, ssd_dt_bias[0])]), mode="ssd")
    dskip = jnp.repeat(ssd_d[0].astype(F32), SSD_P)[None, :]
    y = _ssd_scan(xbc, zs, _per_head(slab, 2, SSD_G, SSD_H // SSD_G), dskip,
                  ssd_norm_w[0][None, :])
    return _out_proj(y, ssd_w_out[0].astype(BF16), x, final_norm_w[None, :], final=True)
```

```python
import functools

import jax
import jax.numpy as jnp
from jax import lax
from jax.experimental import pallas as pl
from jax.experimental.pallas import tpu as pltpu

F32 = jnp.float32
BF16 = jnp.bfloat16

EPS = 1e-6
CONV_K = 4
CONV_C = 4096
Z_W = 2048
SLAB_W = 128

GDN_H_QK = 8
GDN_H_V = 16
GDN_D = 128
GDN_CHUNK = 64

SSD_H = 32
SSD_P = 64
SSD_N = 128
SSD_G = 8
SSD_CHUNK = 128

HALO = 8
VMEM_LIMIT = 56 * 1024 * 1024

_NT = (((1,), (1,)), ((), ()))
_TN = (((0,), (0,)), ((), ()))


def _dot(a, b):
    return jnp.dot(a, b, preferred_element_type=F32)


NEG_LOG2E = -1.4426950408889634


def _sigmoid(x):
    return 1.0 / (1.0 + jnp.exp2(x * NEG_LOG2E))


def _silu(x):
    return x * _sigmoid(x)


def _softplus(x):
    return jnp.maximum(x, 0.0) + jnp.log1p(jnp.exp(-jnp.abs(x)))


def _rms_scale(x):
    return lax.rsqrt(jnp.mean(x * x, axis=-1, keepdims=True) + EPS)


def _chunk_cumsum(g, chunk):
    row = lax.broadcasted_iota(jnp.int32, g.shape, 0) & (chunk - 1)
    s = 1
    while s < chunk:
        g = g + jnp.where(row >= s, pltpu.roll(g, s, 0), 0.0)
        s *= 2
    return g


def _in_kernel(x_ref, nw_ref, wm_ref, ws_ref, cw_ref, cb_ref, p1_ref, p2_ref,
               conv_ref, z_ref, slab_ref, halo_ref, *, mode, tt, cw):
    conv0, z0 = (0, CONV_C) if mode == "gdn" else (Z_W, 0)

    @pl.when(pl.program_id(1) == 0)
    def _():
        halo_ref[...] = jnp.zeros_like(halo_ref)

    x = x_ref[...]
    h = (x * _rms_scale(x) * nw_ref[...]).astype(BF16)

    s = _dot(h, ws_ref[...])
    lane = lax.broadcasted_iota(jnp.int32, s.shape, 1)
    if mode == "gdn":
        n1, n2, chunk = GDN_H_V, 2 * GDN_H_V, GDN_CHUNK
        f1 = _sigmoid(s)
        f2 = -jnp.exp(p1_ref[...]) * _softplus(s + p2_ref[...])
    else:
        n1, n2, chunk = SSD_H, 2 * SSD_H, SSD_CHUNK
        f1 = _softplus(s + p2_ref[...])
        f2 = -jnp.exp(p1_ref[...]) * f1
    g = jnp.where((lane >= n1) & (lane < n2), f2, 0.0)
    slab_ref[...] = jnp.where(lane < n1, f1, _chunk_cumsum(g, chunk))

    row8 = lax.broadcasted_iota(jnp.int32, (HALO, cw), 0)

    def delay(a, a_prev, k):
        r = pltpu.roll(a, k, 0)
        head = jnp.where(row8 < k, pltpu.roll(a_prev, k, 0), r[0:HALO])
        return jnp.concatenate([head, r[HALO:]], axis=0)

    for c in range(CONV_C // cw):
        cols = slice(c * cw, (c + 1) * cw)
        p = _dot(h, wm_ref[:, conv0 + c * cw:conv0 + (c + 1) * cw])
        p_prev = halo_ref[:, cols]
        halo_ref[:, cols] = p[tt - HALO:tt, :]
        w0, w1, w2, w3 = (cw_ref[j:j + 1, cols] for j in range(CONV_K))
        pd = delay(p, p_prev, 1)
        f = p * w1 + pd * w0
        f_prev = p_prev * w1 + pltpu.roll(p_prev, 1, 0) * w0
        y = (p * w3 + pd * w2) + delay(f, f_prev, 2)
        if mode == "ssd":
            y = y + cb_ref[:, cols]
        y = _silu(y)
        if mode == "gdn":
            for hh in range(cw // GDN_D):
                col0 = c * cw + hh * GDN_D
                yh = y[:, hh * GDN_D:(hh + 1) * GDN_D]
                if col0 < 2 * GDN_H_QK * GDN_D:
                    scale = lax.rsqrt(jnp.sum(yh * yh, axis=-1, keepdims=True) + EPS)
                    if col0 < GDN_H_QK * GDN_D:
                        scale = scale * (GDN_D ** -0.5)
                    yh = yh * scale
                conv_ref[:, col0:col0 + GDN_D] = yh.astype(BF16)
        else:
            conv_ref[:, cols] = y.astype(BF16)

    for c in range(Z_W // cw):
        p = _dot(h, wm_ref[:, z0 + c * cw:z0 + (c + 1) * cw])
        z_ref[:, c * cw:(c + 1) * cw] = _silu(p).astype(BF16)


def _in_proj(x, nw, w_main, w_small, conv_w, conv_b, p1, p2, *, mode, tt=512, cw=1024):
    b, t, d = x.shape
    const = lambda i, j: (0, 0)
    tile = lambda i, j: (i, j, 0)
    return pl.pallas_call(
        functools.partial(_in_kernel, mode=mode, tt=tt, cw=cw),
        grid=(b, t // tt),
        in_specs=[
            pl.BlockSpec((None, tt, d), tile),
            pl.BlockSpec((1, d), const),
            pl.BlockSpec(w_main.shape, const, pipeline_mode=pl.Buffered(1)),
            pl.BlockSpec((d, SLAB_W), const),
            pl.BlockSpec((CONV_K, CONV_C), const),
            pl.BlockSpec((1, CONV_C), const),
            pl.BlockSpec((1, SLAB_W), const),
            pl.BlockSpec((1, SLAB_W), const),
        ],
        out_specs=[
            pl.BlockSpec((None, tt, CONV_C), tile),
            pl.BlockSpec((None, tt, Z_W), tile),
            pl.BlockSpec((None, tt, SLAB_W), tile),
        ],
        out_shape=[
            jax.ShapeDtypeStruct((b, t, CONV_C), BF16),
            jax.ShapeDtypeStruct((b, t, Z_W), BF16),
            jax.ShapeDtypeStruct((b, t, SLAB_W), F32),
        ],
        scratch_shapes=[pltpu.VMEM((HALO, CONV_C), F32)],
        compiler_params=pltpu.CompilerParams(
            dimension_semantics=("arbitrary", "arbitrary"), vmem_limit_bytes=VMEM_LIMIT),
        name=f"in_proj_{mode}",
    )(x, nw, w_main, w_small, conv_w, conv_b, p1, p2)


def _out_kernel(y_ref, w_ref, x_ref, fw_ref, o_ref, *, final):
    acc = x_ref[...] + _dot(y_ref[...], w_ref[...])
    if final:
        acc = acc * _rms_scale(acc) * fw_ref[...]
    o_ref[...] = acc


def _out_proj(y, w, x, fw, *, final, tt=512):
    b, t, d = x.shape
    k = y.shape[-1]
    const = lambda i, j: (0, 0)
    tile = lambda i, j: (i, j, 0)
    return pl.pallas_call(
        functools.partial(_out_kernel, final=final),
        grid=(b, t // tt),
        in_specs=[
            pl.BlockSpec((None, tt, k), tile),
            pl.BlockSpec((k, d), const),
            pl.BlockSpec((None, tt, d), tile),
            pl.BlockSpec((1, d), const),
        ],
        out_specs=pl.BlockSpec((None, tt, d), tile),
        out_shape=jax.ShapeDtypeStruct((b, t, d), F32),
        compiler_params=pltpu.CompilerParams(
            dimension_semantics=("arbitrary", "arbitrary"), vmem_limit_bytes=VMEM_LIMIT),
        name="out_proj_final" if final else "out_proj",
    )(y, w, x, fw)


def _gdn_kernel(q_ref, k_ref, v_ref, gb_ref, z_ref, nw_ref, y_ref, s_ref, *, tc, nb, nh):
    C, D = GDN_CHUNK, GDN_D

    @pl.when(pl.program_id(1) == 0)
    def _():
        s_ref[...] = jnp.zeros_like(s_ref)

    nc = tc // C
    units = [(b, g) for b in range(nb) for g in range(nh)]
    pairs = [(c, u) for c in range(nc) for u in units]
    chains = [(c, u, e) for c in range(nc) for u in units for e in range(2)]
    ue = [(u, e) for u in units for e in range(2)]
    rows = lambda c: slice(c * C, (c + 1) * C)
    qk_lanes = lambda u: slice(u[1] * D, (u[1] + 1) * D)
    v_lanes = lambda u, e: slice((2 * u[1] + e) * D, (2 * u[1] + e + 1) * D)
    nw = nw_ref[...]

    head_shift = (SLAB_W - 2 * nh * pl.program_id(0)) & (SLAB_W - 1)
    row = lax.broadcasted_iota(jnp.int32, (C, 2 * C), 0)
    lane = lax.broadcasted_iota(jnp.int32, (C, 2 * C), 1)
    col = lane & (C - 1)
    tril, strict, eye = row >= col, row > col, row == col
    hi = lane >= C
    ident_hi = jnp.where(hi & eye, 1.0, 0.0)
    zero_rows = jnp.zeros((C, 2 * D), BF16)

    kf, qk, kk, g4 = {}, {}, {}, {}
    for c, u in pairs:
        g4[c, u] = pltpu.roll(gb_ref[u[0], rows(c), :], head_shift, 1)
        kb = k_ref[u[0], rows(c), qk_lanes(u)]
        qb = q_ref[u[0], rows(c), qk_lanes(u)]
        kf[c, u] = kb.astype(F32)
        qk_kk = lax.dot_general(jnp.concatenate([qb, kb], axis=0),
                                jnp.concatenate([kb, kb], axis=0), _NT,
                                preferred_element_type=F32)
        qk[c, u], kk[c, u] = qk_kk[:C], qk_kk[C:]

    beta, gc, attn, zk = {}, {}, {}, {}
    for c, u, e in chains:
        ln = 2 * u[1] + e
        beta[c, u, e] = g4[c, u][:, ln:ln + 1]
        gc[c, u, e] = gcv = g4[c, u][:, GDN_H_V + ln:GDN_H_V + ln + 1]
        gc_row = jnp.sum(jnp.where(eye, gcv, 0.0), axis=0, keepdims=True)
        decay = jnp.exp(jnp.where(tril, gcv - gc_row, -jnp.inf))
        nmat = jnp.where(strict, kk[c, u] * decay, 0.0) * beta[c, u, e]
        attn[c, u, e] = (qk[c, u] * decay)[:, :C]
        zk[c, u, e] = jnp.where(hi, ident_hi, -nmat)

    for _ in range(6):
        for ch in chains:
            zb = zk[ch].astype(BF16)
            zk[ch] = _dot(zb[:, :C], zb) + jnp.where(hi, zk[ch], 0.0)

    u_sol, wq, akd, g_end = {}, {}, {}, {}
    for c, u, e in chains:
        bt, gcv = beta[c, u, e], gc[c, u, e]
        eg = jnp.exp(gcv)
        vf = v_ref[u[0], rows(c), v_lanes(u, e)].astype(F32)
        rhs = jnp.concatenate([vf * bt, kf[c, u] * (bt * eg)], axis=1).astype(BF16)
        sol = _dot(zk[c, u, e].astype(BF16), jnp.concatenate([zero_rows, rhs], axis=0))
        u_sol[c, u, e] = sol[:, :D]
        qd = q_ref[u[0], rows(c), qk_lanes(u)].astype(F32) * eg
        wq[c, u, e] = jnp.concatenate([sol[:, D:], qd], axis=0).astype(BF16)
        g_last = gcv[C - 1:C, :]
        kd = kf[c, u] * jnp.exp(g_last - gcv)
        akd[c, u, e] = jnp.concatenate([attn[c, u, e], kd.T], axis=0).astype(BF16)
        g_end[c, u, e] = jnp.exp(g_last)

    states = {(u, e): s_ref[u[0], 2 * u[1] + e] for u, e in ue}
    for c in range(nc):
        ws = {k: _dot(wq[(c,) + k], states[k].astype(BF16)) for k in ue}
        av = {}
        for k in ue:
            v_new = (u_sol[(c,) + k] - ws[k][:C]).astype(BF16)
            av[k] = _dot(akd[(c,) + k], v_new)
            states[k] = states[k] * g_end[(c,) + k] + av[k][C:]
        for u, e in ue:
            o = ws[u, e][C:] + av[u, e][:C]
            gate = z_ref[u[0], rows(c), v_lanes(u, e)].astype(F32)
            y_ref[u[0], rows(c), v_lanes(u, e)] = (o * _rms_scale(o) * nw * gate).astype(BF16)
    for u, e in ue:
        s_ref[u[0], 2 * u[1] + e] = states[u, e]


def _gdn_scan(qkv, zs, slab, nw, *, tc=256, nh=2):
    b, t, _ = qkv.shape
    d = GDN_D
    ng = GDN_H_QK // nh
    return pl.pallas_call(
        functools.partial(_gdn_kernel, tc=tc, nb=b, nh=nh),
        grid=(ng, t // tc),
        in_specs=[
            pl.BlockSpec((b, tc, nh * d), lambda h, j: (0, j, h)),
            pl.BlockSpec((b, tc, nh * d), lambda h, j: (0, j, ng + h)),
            pl.BlockSpec((b, tc, 2 * nh * d), lambda h, j: (0, j, ng + h)),
            pl.BlockSpec((b, tc, SLAB_W), lambda h, j: (0, j, 0)),
            pl.BlockSpec((b, tc, 2 * nh * d), lambda h, j: (0, j, h)),
            pl.BlockSpec((1, d), lambda h, j: (0, 0)),
        ],
        out_specs=pl.BlockSpec((b, tc, 2 * nh * d), lambda h, j: (0, j, h)),
        out_shape=jax.ShapeDtypeStruct((b, t, GDN_H_V * d), BF16),
        scratch_shapes=[pltpu.VMEM((b, 2 * nh, d, d), F32)],
        compiler_params=pltpu.CompilerParams(
            dimension_semantics=("arbitrary", "arbitrary"), vmem_limit_bytes=VMEM_LIMIT),
        name="gdn_scan",
    )(qkv, qkv, qkv, slab, zs, nw)


def _ssd_kernel(xs_ref, b_ref, c_ref, z_ref, sl_ref, d_ref, nw_ref, y_ref, st_ref, *, tc, nb):
    Q, W = SSD_CHUNK, 2 * SSD_P

    @pl.when(pl.program_id(1) == 0)
    def _():
        st_ref[...] = jnp.zeros_like(st_ref)

    row = lax.broadcasted_iota(jnp.int32, (Q, Q), 0)
    col = lax.broadcasted_iota(jnp.int32, (Q, Q), 1)
    tril, eye = row >= col, row == col
    lo = lax.broadcasted_iota(jnp.int32, (Q, W), 1) < SSD_P
    nw = nw_ref[...]
    dskip = d_ref[...]
    heads = SSD_H // SSD_G
    head_shift = (SLAB_W - heads * pl.program_id(0)) & (SLAB_W - 1)
    nc = tc // Q
    pairs = [(c, b) for c in range(nc) for b in range(nb)]
    rows = lambda c: slice(c * Q, (c + 1) * Q)
    lanes = lambda p: slice(p * W, (p + 1) * W)

    cb, bt, s8, s8t = {}, {}, {}, {}
    for c, b in pairs:
        s8[c, b] = pltpu.roll(sl_ref[b, rows(c), :], head_shift, 1)
        s8t[c, b] = s8[c, b].T
        bm = b_ref[b, rows(c), :]
        cb[c, b] = lax.dot_general(c_ref[b, rows(c), :], bm, _NT,
                                   preferred_element_type=F32)
        bt[c, b] = bm.astype(F32).T

    mats, bts, acl, cdec = {}, {}, {}, {}
    for c, b in pairs:
        cds = []
        for p in range(2):
            cols = []
            for i in range(2):
                r = 2 * p + i
                dt_row = s8t[c, b][r:r + 1, :]
                ac_row = s8t[c, b][SSD_H + r:SSD_H + r + 1, :]
                ac_col = s8[c, b][:, SSD_H + r:SSD_H + r + 1]
                cols.append(ac_col)
                decay = jnp.exp(jnp.where(tril, ac_col - ac_row, -jnp.inf))
                mats[c, b, p, i] = (cb[c, b] * decay * dt_row).astype(BF16)
                last = ac_col[Q - 1:Q, :]
                bts[c, b, p, i] = (bt[c, b] * (dt_row * jnp.exp(last - ac_row))).astype(BF16)
            acl[c, b, p] = jnp.where(lo, cols[0], cols[1])
            cds.append(jnp.exp(jnp.where(lo[0:1], cols[0][Q - 1:Q, :], cols[1][Q - 1:Q, :])))
        cdec[c, b] = jnp.concatenate(cds, axis=1)

    xp = {(c, b, p): xs_ref[b, rows(c), lanes(p)] for c, b in pairs for p in range(2)}
    yd = {k: _dot(m, xp[k[:3]]) for k, m in mats.items()}
    up = {k: _dot(m, xp[k[:3]]) for k, m in bts.items()}
    upd = {(c, b): jnp.concatenate([jnp.where(lo, up[c, b, p, 0], up[c, b, p, 1])
                                    for p in range(2)], axis=1) for c, b in pairs}

    prev = {}
    for b in range(nb):
        state = st_ref[b]
        for c in range(nc):
            prev[c, b] = state.astype(BF16)
            state = state * cdec[c, b] + upd[c, b]
        st_ref[b] = state
    y_off = {k: _dot(c_ref[k[1], rows(k[0]), :], prev[k]) for k in pairs}

    for c, b in pairs:
        ys = []
        for p in range(2):
            ys.append(jnp.where(lo, yd[c, b, p, 0], yd[c, b, p, 1])
                      + y_off[c, b][:, lanes(p)] * jnp.exp(acl[c, b, p])
                      + xp[c, b, p].astype(F32) * dskip[:, lanes(p)])
        y = jnp.concatenate(ys, axis=1) * z_ref[b, rows(c), :].astype(F32)
        y_ref[b, rows(c), :] = (y * _rms_scale(y) * nw).astype(BF16)


def _ssd_scan(xbc, zs, sl, dskip, nw, *, tc=512):
    b, t, _ = xbc.shape
    g, n = SSD_G, SSD_N
    gw = SSD_H // SSD_G * SSD_P
    xw = SSD_H * SSD_P // n
    return pl.pallas_call(
        functools.partial(_ssd_kernel, tc=tc, nb=b),
        grid=(g, t // tc),
        in_specs=[
            pl.BlockSpec((b, tc, gw), lambda h, j: (0, j, h)),
            pl.BlockSpec((b, tc, n), lambda h, j: (0, j, xw + h)),
            pl.BlockSpec((b, tc, n), lambda h, j: (0, j, xw + g + h)),
            pl.BlockSpec((b, tc, gw), lambda h, j: (0, j, h)),
            pl.BlockSpec((b, tc, SLAB_W), lambda h, j: (0, j, 0)),
            pl.BlockSpec((1, gw), lambda h, j: (0, h)),
            pl.BlockSpec((1, gw), lambda h, j: (0, h)),
        ],
        out_specs=pl.BlockSpec((b, tc, gw), lambda h, j: (0, j, h)),
        out_shape=jax.ShapeDtypeStruct((b, t, SSD_H * SSD_P), BF16),
        scratch_shapes=[pltpu.VMEM((b, n, gw), F32)],
        compiler_params=pltpu.CompilerParams(
            dimension_semantics=("arbitrary", "arbitrary"), vmem_limit_bytes=VMEM_LIMIT),
        name="ssd_scan",
    )(xbc, xbc, xbc, zs, sl, dskip, nw)


def _lane_row(pieces):
    row = jnp.zeros((SLAB_W,), F32)
    for off, v in pieces:
        row = row.at[off:off + v.shape[0]].set(v.astype(F32))
    return row[None, :]


def kernel(x, norm_w, gdn_w_in, gdn_conv_w, gdn_a_log, gdn_dt_bias, gdn_norm_w, gdn_w_out,
           ssd_w_in, ssd_conv_w, ssd_conv_b, ssd_dt_bias, ssd_a_log, ssd_d, ssd_norm_w,
           ssd_w_out, final_norm_w):
    d = x.shape[-1]
    hv, h = GDN_H_V, SSD_H

    w = gdn_w_in[0].astype(BF16)
    w_small = jnp.pad(w[:, CONV_C + Z_W:], ((0, 0), (0, SLAB_W - 2 * hv)))
    qkv, zs, slab = _in_proj(
        x, norm_w[0:1], w, w_small, gdn_conv_w[0],
        jnp.zeros((1, CONV_C), F32), _lane_row([(hv, gdn_a_log[0])]),
        _lane_row([(hv, gdn_dt_bias[0])]), mode="gdn")
    y = _gdn_scan(qkv, zs, slab, gdn_norm_w[0][None, :])
    x = _out_proj(y, gdn_w_out[0].astype(BF16), x, final_norm_w[None, :], final=False)

    w = ssd_w_in[0].astype(BF16)
    w_dt = w[:, Z_W + CONV_C:]
    w_small = jnp.pad(jnp.concatenate([w_dt, w_dt], axis=1), ((0, 0), (0, SLAB_W - 2 * h)))
    xbc, zs, slab = _in_proj(
        x, norm_w[1:2], w, w_small, ssd_conv_w[0], ssd_conv_b[0][None, :],
        _lane_row([(h, ssd_a_log[0])]),
        _lane_row([(0, ssd_dt_bias[0]), (h, ssd_dt_bias[0])]), mode="ssd")
    dskip = jnp.repeat(ssd_d[0].astype(F32), SSD_P)[None, :]
    y = _ssd_scan(xbc, zs, slab, dskip, ssd_norm_w[0][None, :])
    return _out_proj(y, ssd_w_out[0].astype(BF16), x, final_norm_w[None, :], final=True)
```

```python
import functools

import jax
import jax.numpy as jnp
from jax import lax
from jax.experimental import pallas as pl
from jax.experimental.pallas import tpu as pltpu

F32 = jnp.float32
BF16 = jnp.bfloat16

EPS = 1e-6
CONV_K = 4
CONV_C = 4096
Z_W = 2048
SLAB_W = 128

GDN_H_QK = 8
GDN_H_V = 16
GDN_D = 128
GDN_CHUNK = 64

SSD_H = 32
SSD_P = 64
SSD_N = 128
SSD_G = 8
SSD_CHUNK = 128

HALO = 8
VMEM_LIMIT = 56 * 1024 * 1024

_NT = (((1,), (1,)), ((), ()))
_TN = (((0,), (0,)), ((), ()))


def _dot(a, b):
    return jnp.dot(a, b, preferred_element_type=F32)


NEG_LOG2E = -1.4426950408889634


def _sigmoid(x):
    return 1.0 / (1.0 + jnp.exp2(x * NEG_LOG2E))


def _silu(x):
    h = 0.5 * x
    return h + h * jnp.tanh(h)


def _softplus(x):
    return jnp.maximum(x, 0.0) + jnp.log1p(jnp.exp(-jnp.abs(x)))


def _rms_scale(x):
    return lax.rsqrt(jnp.mean(x * x, axis=-1, keepdims=True) + EPS)


def _chunk_cumsum(g, chunk):
    row = lax.broadcasted_iota(jnp.int32, g.shape, 0) & (chunk - 1)
    s = 1
    while s < chunk:
        g = g + jnp.where(row >= s, pltpu.roll(g, s, 0), 0.0)
        s *= 2
    return g


def _in_kernel(x_ref, nw_ref, wm_ref, ws_ref, cw_ref, cb_ref, p1_ref, p2_ref,
               conv_ref, z_ref, slab_ref, halo_ref, *, mode, tt, cw):
    conv0, z0 = (0, CONV_C) if mode == "gdn" else (Z_W, 0)

    @pl.when(pl.program_id(1) == 0)
    def _():
        halo_ref[...] = jnp.zeros_like(halo_ref)

    x = x_ref[...]
    h = (x * _rms_scale(x) * nw_ref[...]).astype(BF16)

    s = _dot(h, ws_ref[...])
    lane = lax.broadcasted_iota(jnp.int32, s.shape, 1)
    if mode == "gdn":
        n1, n2, chunk = GDN_H_V, 2 * GDN_H_V, GDN_CHUNK
        f1 = _sigmoid(s)
        f2 = -jnp.exp(p1_ref[...]) * _softplus(s + p2_ref[...])
    else:
        n1, n2, chunk = SSD_H, 2 * SSD_H, SSD_CHUNK
        f1 = _softplus(s + p2_ref[...])
        f2 = -jnp.exp(p1_ref[...]) * f1
    g = jnp.where((lane >= n1) & (lane < n2), f2, 0.0)
    slab_ref[...] = jnp.where(lane < n1, f1, _chunk_cumsum(g, chunk))

    row8 = lax.broadcasted_iota(jnp.int32, (HALO, cw), 0)

    def delay(a, a_prev, k):
        r = pltpu.roll(a, k, 0)
        head = jnp.where(row8 < k, pltpu.roll(a_prev, k, 0), r[0:HALO])
        return jnp.concatenate([head, r[HALO:]], axis=0)

    for c in range(CONV_C // cw):
        cols = slice(c * cw, (c + 1) * cw)
        p = _dot(h, wm_ref[:, conv0 + c * cw:conv0 + (c + 1) * cw])
        p_prev = halo_ref[:, cols]
        halo_ref[:, cols] = p[tt - HALO:tt, :]
        w0, w1, w2, w3 = (cw_ref[j:j + 1, cols] for j in range(CONV_K))
        pd = delay(p, p_prev, 1)
        f = p * w1 + pd * w0
        f_prev = p_prev * w1 + pltpu.roll(p_prev, 1, 0) * w0
        y = (p * w3 + pd * w2) + delay(f, f_prev, 2)
        if mode == "ssd":
            y = y + cb_ref[:, cols]
        y = _silu(y)
        if mode == "gdn":
            for hh in range(cw // GDN_D):
                col0 = c * cw + hh * GDN_D
                yh = y[:, hh * GDN_D:(hh + 1) * GDN_D]
                if col0 < 2 * GDN_H_QK * GDN_D:
                    scale = lax.rsqrt(jnp.sum(yh * yh, axis=-1, keepdims=True) + EPS)
                    if col0 < GDN_H_QK * GDN_D:
                        scale = scale * (GDN_D ** -0.5)
                    yh = yh * scale
                conv_ref[:, col0:col0 + GDN_D] = yh.astype(BF16)
        else:
            conv_ref[:, cols] = y.astype(BF16)

    for c in range(Z_W // cw):
        p = _dot(h, wm_ref[:, z0 + c * cw:z0 + (c + 1) * cw])
        z_ref[:, c * cw:(c + 1) * cw] = _silu(p).astype(BF16)


def _in_proj(x, nw, w_main, w_small, conv_w, conv_b, p1, p2, *, mode, tt=512, cw=1024):
    b, t, d = x.shape
    const = lambda i, j: (0, 0)
    tile = lambda i, j: (i, j, 0)
    return pl.pallas_call(
        functools.partial(_in_kernel, mode=mode, tt=tt, cw=cw),
        grid=(b, t // tt),
        in_specs=[
            pl.BlockSpec((None, tt, d), tile),
            pl.BlockSpec((1, d), const),
            pl.BlockSpec(w_main.shape, const, pipeline_mode=pl.Buffered(1)),
            pl.BlockSpec((d, SLAB_W), const),
            pl.BlockSpec((CONV_K, CONV_C), const),
            pl.BlockSpec((1, CONV_C), const),
            pl.BlockSpec((1, SLAB_W), const),
            pl.BlockSpec((1, SLAB_W), const),
        ],
        out_specs=[
            pl.BlockSpec((None, tt, CONV_C), tile),
            pl.BlockSpec((None, tt, Z_W), tile),
            pl.BlockSpec((None, tt, SLAB_W), tile),
        ],
        out_shape=[
            jax.ShapeDtypeStruct((b, t, CONV_C), BF16),
            jax.ShapeDtypeStruct((b, t, Z_W), BF16),
            jax.ShapeDtypeStruct((b, t, SLAB_W), F32),
        ],
        scratch_shapes=[pltpu.VMEM((HALO, CONV_C), F32)],
        compiler_params=pltpu.CompilerParams(
            dimension_semantics=("arbitrary", "arbitrary"), vmem_limit_bytes=VMEM_LIMIT),
        name=f"in_proj_{mode}",
    )(x, nw, w_main, w_small, conv_w, conv_b, p1, p2)


def _out_kernel(y_ref, w_ref, x_ref, fw_ref, o_ref, *, final):
    acc = x_ref[...] + _dot(y_ref[...], w_ref[...])
    if final:
        acc = acc * _rms_scale(acc) * fw_ref[...]
    o_ref[...] = acc


def _out_proj(y, w, x, fw, *, final, tt=512):
    b, t, d = x.shape
    k = y.shape[-1]
    const = lambda i, j: (0, 0)
    tile = lambda i, j: (i, j, 0)
    return pl.pallas_call(
        functools.partial(_out_kernel, final=final),
        grid=(b, t // tt),
        in_specs=[
            pl.BlockSpec((None, tt, k), tile),
            pl.BlockSpec((k, d), const),
            pl.BlockSpec((None, tt, d), tile),
            pl.BlockSpec((1, d), const),
        ],
        out_specs=pl.BlockSpec((None, tt, d), tile),
        out_shape=jax.ShapeDtypeStruct((b, t, d), F32),
        compiler_params=pltpu.CompilerParams(
            dimension_semantics=("arbitrary", "arbitrary"), vmem_limit_bytes=VMEM_LIMIT),
        name="out_proj_final" if final else "out_proj",
    )(y, w, x, fw)


def _gdn_kernel(q_ref, k_ref, v_ref, gb_ref, z_ref, nw_ref, y_ref, s_ref, *, tc, nb, nh):
    C, D = GDN_CHUNK, GDN_D

    @pl.when(pl.program_id(1) == 0)
    def _():
        s_ref[...] = jnp.zeros_like(s_ref)

    nc = tc // C
    units = [(b, g) for b in range(nb) for g in range(nh)]
    pairs = [(c, u) for c in range(nc) for u in units]
    chains = [(c, u, e) for c in range(nc) for u in units for e in range(2)]
    ue = [(u, e) for u in units for e in range(2)]
    rows = lambda c: slice(c * C, (c + 1) * C)
    qk_lanes = lambda u: slice(u[1] * D, (u[1] + 1) * D)
    v_lanes = lambda u, e: slice((2 * u[1] + e) * D, (2 * u[1] + e + 1) * D)
    nw = nw_ref[...]

    head_shift = (SLAB_W - 2 * nh * pl.program_id(0)) & (SLAB_W - 1)
    row = lax.broadcasted_iota(jnp.int32, (C, 2 * C), 0)
    lane = lax.broadcasted_iota(jnp.int32, (C, 2 * C), 1)
    col = lane & (C - 1)
    tril, strict, eye = row >= col, row > col, row == col
    hi = lane >= C
    ident_hi = jnp.where(hi & eye, 1.0, 0.0)
    zero_rows = jnp.zeros((C, 2 * D), BF16)

    kf, qk, kk, g4 = {}, {}, {}, {}
    for c, u in pairs:
        g4[c, u] = pltpu.roll(gb_ref[u[0], rows(c), :], head_shift, 1)
        kb = k_ref[u[0], rows(c), qk_lanes(u)]
        qb = q_ref[u[0], rows(c), qk_lanes(u)]
        kf[c, u] = kb.astype(F32)
        qk_kk = lax.dot_general(jnp.concatenate([qb, kb], axis=0),
                                jnp.concatenate([kb, kb], axis=0), _NT,
                                preferred_element_type=F32)
        qk[c, u], kk[c, u] = qk_kk[:C], qk_kk[C:]

    beta, gc, attn, zk = {}, {}, {}, {}
    for c, u, e in chains:
        ln = 2 * u[1] + e
        beta[c, u, e] = g4[c, u][:, ln:ln + 1]
        gc[c, u, e] = gcv = g4[c, u][:, GDN_H_V + ln:GDN_H_V + ln + 1]
        gc_row = jnp.sum(jnp.where(eye, gcv, 0.0), axis=0, keepdims=True)
        decay = jnp.exp(jnp.where(tril, gcv - gc_row, -jnp.inf))
        nmat = jnp.where(strict, kk[c, u] * decay, 0.0) * beta[c, u, e]
        attn[c, u, e] = (qk[c, u] * decay)[:, :C]
        zk[c, u, e] = jnp.where(hi, ident_hi, -nmat)

    for _ in range(6):
        for ch in chains:
            zb = zk[ch].astype(BF16)
            zk[ch] = _dot(zb[:, :C], zb) + jnp.where(hi, zk[ch], 0.0)

    u_sol, wq, akd, g_end = {}, {}, {}, {}
    for c, u, e in chains:
        bt, gcv = beta[c, u, e], gc[c, u, e]
        eg = jnp.exp(gcv)
        vf = v_ref[u[0], rows(c), v_lanes(u, e)].astype(F32)
        rhs = jnp.concatenate([vf * bt, kf[c, u] * (bt * eg)], axis=1).astype(BF16)
        sol = _dot(zk[c, u, e].astype(BF16), jnp.concatenate([zero_rows, rhs], axis=0))
        u_sol[c, u, e] = sol[:, :D]
        qd = q_ref[u[0], rows(c), qk_lanes(u)].astype(F32) * eg
        wq[c, u, e] = jnp.concatenate([sol[:, D:], qd], axis=0).astype(BF16)
        g_last = gcv[C - 1:C, :]
        kd = kf[c, u] * jnp.exp(g_last - gcv)
        akd[c, u, e] = jnp.concatenate([attn[c, u, e], kd.T], axis=0).astype(BF16)
        g_end[c, u, e] = jnp.exp(g_last)

    states = {(u, e): s_ref[u[0], 2 * u[1] + e] for u, e in ue}
    for c in range(nc):
        ws = {k: _dot(wq[(c,) + k], states[k].astype(BF16)) for k in ue}
        av = {}
        for k in ue:
            v_new = (u_sol[(c,) + k] - ws[k][:C]).astype(BF16)
            av[k] = _dot(akd[(c,) + k], v_new)
            states[k] = states[k] * g_end[(c,) + k] + av[k][C:]
        for u, e in ue:
            o = ws[u, e][C:] + av[u, e][:C]
            gate = z_ref[u[0], rows(c), v_lanes(u, e)].astype(F32)
            y_ref[u[0], rows(c), v_lanes(u, e)] = (o * _rms_scale(o) * nw * gate).astype(BF16)
    for u, e in ue:
        s_ref[u[0], 2 * u[1] + e] = states[u, e]


def _gdn_scan(qkv, zs, slab, nw, *, tc=128, nh=4):
    b, t, _ = qkv.shape
    d = GDN_D
    ng = GDN_H_QK // nh
    return pl.pallas_call(
        functools.partial(_gdn_kernel, tc=tc, nb=b, nh=nh),
        grid=(ng, t // tc),
        in_specs=[
            pl.BlockSpec((b, tc, nh * d), lambda h, j: (0, j, h)),
            pl.BlockSpec((b, tc, nh * d), lambda h, j: (0, j, ng + h)),
            pl.BlockSpec((b, tc, 2 * nh * d), lambda h, j: (0, j, ng + h)),
            pl.BlockSpec((b, tc, SLAB_W), lambda h, j: (0, j, 0)),
            pl.BlockSpec((b, tc, 2 * nh * d), lambda h, j: (0, j, h)),
            pl.BlockSpec((1, d), lambda h, j: (0, 0)),
        ],
        out_specs=pl.BlockSpec((b, tc, 2 * nh * d), lambda h, j: (0, j, h)),
        out_shape=jax.ShapeDtypeStruct((b, t, GDN_H_V * d), BF16),
        scratch_shapes=[pltpu.VMEM((b, 2 * nh, d, d), F32)],
        compiler_params=pltpu.CompilerParams(
            dimension_semantics=("arbitrary", "arbitrary"), vmem_limit_bytes=VMEM_LIMIT),
        name="gdn_scan",
    )(qkv, qkv, qkv, slab, zs, nw)


def _ssd_kernel(xs_ref, b_ref, c_ref, z_ref, sl_ref, d_ref, nw_ref, y_ref, st_ref, *, tc, nb):
    Q, W = SSD_CHUNK, 2 * SSD_P

    @pl.when(pl.program_id(1) == 0)
    def _():
        st_ref[...] = jnp.zeros_like(st_ref)

    row = lax.broadcasted_iota(jnp.int32, (Q, Q), 0)
    col = lax.broadcasted_iota(jnp.int32, (Q, Q), 1)
    tril, eye = row >= col, row == col
    lo = lax.broadcasted_iota(jnp.int32, (Q, W), 1) < SSD_P
    nw = nw_ref[...]
    dskip = d_ref[...]
    heads = SSD_H // SSD_G
    head_shift = (SLAB_W - heads * pl.program_id(0)) & (SLAB_W - 1)
    nc = tc // Q
    pairs = [(c, b) for c in range(nc) for b in range(nb)]
    rows = lambda c: slice(c * Q, (c + 1) * Q)
    lanes = lambda p: slice(p * W, (p + 1) * W)

    cb, bt, s8, s8t = {}, {}, {}, {}
    for c, b in pairs:
        s8[c, b] = pltpu.roll(sl_ref[b, rows(c), :], head_shift, 1)
        s8t[c, b] = s8[c, b].T
        bm = b_ref[b, rows(c), :]
        cb[c, b] = lax.dot_general(c_ref[b, rows(c), :], bm, _NT,
                                   preferred_element_type=F32)
        bt[c, b] = bm.astype(F32).T

    mats, bts, acl, cdec = {}, {}, {}, {}
    for c, b in pairs:
        cds = []
        for p in range(2):
            cols = []
            for i in range(2):
                r = 2 * p + i
                dt_row = s8t[c, b][r:r + 1, :]
                ac_row = s8t[c, b][SSD_H + r:SSD_H + r + 1, :]
                ac_col = s8[c, b][:, SSD_H + r:SSD_H + r + 1]
                cols.append(ac_col)
                decay = jnp.exp(jnp.where(tril, ac_col - ac_row, -jnp.inf))
                mats[c, b, p, i] = (cb[c, b] * decay * dt_row).astype(BF16)
                last = ac_col[Q - 1:Q, :]
                bts[c, b, p, i] = (bt[c, b] * (dt_row * jnp.exp(last - ac_row))).astype(BF16)
            acl[c, b, p] = jnp.where(lo, cols[0], cols[1])
            cds.append(jnp.exp(jnp.where(lo[0:1], cols[0][Q - 1:Q, :], cols[1][Q - 1:Q, :])))
        cdec[c, b] = jnp.concatenate(cds, axis=1)

    xp = {(c, b, p): xs_ref[b, rows(c), lanes(p)] for c, b in pairs for p in range(2)}
    yd = {k: _dot(m, xp[k[:3]]) for k, m in mats.items()}
    up = {k: _dot(m, xp[k[:3]]) for k, m in bts.items()}
    upd = {(c, b): jnp.concatenate([jnp.where(lo, up[c, b, p, 0], up[c, b, p, 1])
                                    for p in range(2)], axis=1) for c, b in pairs}

    prev = {}
    for b in range(nb):
        state = st_ref[b]
        for c in range(nc):
            prev[c, b] = state.astype(BF16)
            state = state * cdec[c, b] + upd[c, b]
        st_ref[b] = state
    y_off = {k: _dot(c_ref[k[1], rows(k[0]), :], prev[k]) for k in pairs}

    for c, b in pairs:
        ys = []
        for p in range(2):
            ys.append(jnp.where(lo, yd[c, b, p, 0], yd[c, b, p, 1])
                      + y_off[c, b][:, lanes(p)] * jnp.exp(acl[c, b, p])
                      + xp[c, b, p].astype(F32) * dskip[:, lanes(p)])
        y = jnp.concatenate(ys, axis=1) * z_ref[b, rows(c), :].astype(F32)
        y_ref[b, rows(c), :] = (y * _rms_scale(y) * nw).astype(BF16)


def _ssd_scan(xbc, zs, sl, dskip, nw, *, tc=1024):
    b, t, _ = xbc.shape
    g, n = SSD_G, SSD_N
    gw = SSD_H // SSD_G * SSD_P
    xw = SSD_H * SSD_P // n
    return pl.pallas_call(
        functools.partial(_ssd_kernel, tc=tc, nb=b),
        grid=(g, t // tc),
        in_specs=[
            pl.BlockSpec((b, tc, gw), lambda h, j: (0, j, h)),
            pl.BlockSpec((b, tc, n), lambda h, j: (0, j, xw + h)),
            pl.BlockSpec((b, tc, n), lambda h, j: (0, j, xw + g + h)),
            pl.BlockSpec((b, tc, gw), lambda h, j: (0, j, h)),
            pl.BlockSpec((b, tc, SLAB_W), lambda h, j: (0, j, 0)),
            pl.BlockSpec((1, gw), lambda h, j: (0, h)),
            pl.BlockSpec((1, gw), lambda h, j: (0, h)),
        ],
        out_specs=pl.BlockSpec((b, tc, gw), lambda h, j: (0, j, h)),
        out_shape=jax.ShapeDtypeStruct((b, t, SSD_H * SSD_P), BF16),
        scratch_shapes=[pltpu.VMEM((b, n, gw), F32)],
        compiler_params=pltpu.CompilerParams(
            dimension_semantics=("arbitrary", "arbitrary"), vmem_limit_bytes=VMEM_LIMIT),
        name="ssd_scan",
    )(xbc, xbc, xbc, zs, sl, dskip, nw)


def _lane_row(pieces):
    row = jnp.zeros((SLAB_W,), F32)
    for off, v in pieces:
        row = row.at[off:off + v.shape[0]].set(v.astype(F32))
    return row[None, :]


def kernel(x, norm_w, gdn_w_in, gdn_conv_w, gdn_a_log, gdn_dt_bias, gdn_norm_w, gdn_w_out,
           ssd_w_in, ssd_conv_w, ssd_conv_b, ssd_dt_bias, ssd_a_log, ssd_d, ssd_norm_w,
           ssd_w_out, final_norm_w):
    d = x.shape[-1]
    hv, h = GDN_H_V, SSD_H

    w = gdn_w_in[0].astype(BF16)
    w_small = jnp.pad(w[:, CONV_C + Z_W:], ((0, 0), (0, SLAB_W - 2 * hv)))
    qkv, zs, slab = _in_proj(
        x, norm_w[0:1], w, w_small, gdn_conv_w[0],
        jnp.zeros((1, CONV_C), F32), _lane_row([(hv, gdn_a_log[0])]),
        _lane_row([(hv, gdn_dt_bias[0])]), mode="gdn")
    y = _gdn_scan(qkv, zs, slab, gdn_norm_w[0][None, :])
    x = _out_proj(y, gdn_w_out[0].astype(BF16), x, final_norm_w[None, :], final=False)

    w = ssd_w_in[0].astype(BF16)
    w_dt = w[:, Z_W + CONV_C:]
    w_small = jnp.pad(jnp.concatenate([w_dt, w_dt], axis=1), ((0, 0), (0, SLAB_W - 2 * h)))
    xbc, zs, slab = _in_proj(
        x, norm_w[1:2], w, w_small, ssd_conv_w[0], ssd_conv_b[0][None, :],
        _lane_row([(h, ssd_a_log[0])]),
        _lane_row([(0, ssd_dt_bias[0]), (h, ssd_dt_bias[0])]), mode="ssd")
    dskip = jnp.repeat(ssd_d[0].astype(F32), SSD_P)[None, :]
    y = _ssd_scan(xbc, zs, slab, dskip, ssd_norm_w[0][None, :])
    return _out_proj(y, ssd_w_out[0].astype(BF16), x, final_norm_w[None, :], final=True)
```
